```python
import math
import jax
import jax.numpy as jnp
from jax import lax
import numpy as np

D_MODEL = 1024
BATCH = 4
SEQ = 8192
DEPTH = 2
DEC_BATCH = 128
DEC_SEQ = 8
PAST_LEN = 16384
PAGE_SIZE = 128

ROPE_THETA = 10000.0
RMS_EPS = 1e-6
NEG_INF = -1e30

MLA_HEADS = 8
MLA_Q_LORA = 256
MLA_KV_LORA = 128
MLA_NOPE = 64
MLA_ROPE = 32
MLA_V = 64
MLA_QBLOCK = 128

MOBA_HEADS = 8
MOBA_KV_HEADS = 4
MOBA_GROUP = MOBA_HEADS // MOBA_KV_HEADS
MOBA_HD = 64
MOBA_BLOCK = 256
MOBA_TOPK = 3
MOBA_QCHUNK = 32

MLSTM_HEADS = 8
MLSTM_DK = 64
MLSTM_DV = 128
MLSTM_CHUNK = 64

D_FF = 2816
N_EXPERTS = 8
MOE_TOPK = 2
D_FF_EXPERT = 3584
MOE_BLOCK = 256

IN0_SPLITS = (MLA_Q_LORA, MLA_KV_LORA, MLA_ROPE, MOBA_HEADS * MOBA_HD, MOBA_KV_HEADS * MOBA_HD, MOBA_KV_HEADS * MOBA_HD)
IN0_WIDTH = sum(IN0_SPLITS)
MIX0_OUT = MLA_HEADS * MLA_V + MOBA_HEADS * MOBA_HD
IN1_SPLITS = (MLSTM_HEADS * MLSTM_DK, MLSTM_HEADS * MLSTM_DK, MLSTM_HEADS * MLSTM_DV, MLSTM_HEADS * MLSTM_DV, 2 * MLSTM_HEADS)
IN1_WIDTH = sum(IN1_SPLITS)

kernel_name = 'mla_moba_mlstm_hybrid_step'


def _split(z, sizes):
    idx, acc = [], 0
    for s in sizes[:-1]:
        acc += s
        idx.append(acc)
    return jnp.split(z, idx, axis=-1)


def rmsnorm(x, g):
    x32 = x.astype(jnp.float32)
    y = x32 * lax.rsqrt(jnp.mean(x32 * x32, axis=-1, keepdims=True) + RMS_EPS)
    return (y * g.astype(jnp.float32)).astype(x.dtype)


def adaln(x, c, g, w, b):
    mod = (c @ w + b).astype(jnp.float32)
    shift, scale, gate = jnp.split(mod, 3, axis=-1)
    x32 = x.astype(jnp.float32)
    y = x32 * lax.rsqrt(jnp.mean(x32 * x32, axis=-1, keepdims=True) + RMS_EPS) * g.astype(jnp.float32)
    h = y * (1.0 + scale[:, None, :]) + shift[:, None, :]
    return h.astype(x.dtype), gate[:, None, :].astype(x.dtype)


def rope(x, pos):
    d = x.shape[-1]
    inv = jnp.power(ROPE_THETA, -jnp.arange(0, d, 2, dtype=jnp.float32) / d)
    ang = pos[:, None] * inv[None, :]
    cos = jnp.cos(ang)[None, :, None, :]
    sin = jnp.sin(ang)[None, :, None, :]
    x32 = x.astype(jnp.float32)
    x1, x2 = x32[..., : d // 2], x32[..., d // 2:]
    return jnp.concatenate([x1 * cos - x2 * sin, x2 * cos + x1 * sin], axis=-1).astype(x.dtype)


def swiglu(h, w1, w3, w2):
    return (jax.nn.silu(h @ w1) * (h @ w3)) @ w2


def even_project(h, pos, w_in0, mla_q_norm_g, mla_w_uq, mla_kv_norm_g, mla_w_uk):
    B, T, _ = h.shape
    cq, ckv, kr, mq, mk, mv = _split(h @ w_in0, IN0_SPLITS)
    q = (rmsnorm(cq, mla_q_norm_g) @ mla_w_uq).reshape(B, T, MLA_HEADS, MLA_NOPE + MLA_ROPE)
    q_nope = q[..., :MLA_NOPE]
    q_rope = rope(q[..., MLA_NOPE:], pos)
    q_lat = jnp.einsum('bthn,chn->bthc', q_nope, mla_w_uk)
    latent = rmsnorm(ckv, mla_kv_norm_g)
    k_rope = rope(kr[:, :, None, :], pos)[:, :, 0, :]
    mq = rope(mq.reshape(B, T, MOBA_HEADS, MOBA_HD), pos)
    mk = rope(mk.reshape(B, T, MOBA_KV_HEADS, MOBA_HD), pos)
    mv = mv.reshape(B, T, MOBA_KV_HEADS, MOBA_HD)
    return q_lat, q_rope, latent, k_rope, mq, mk, mv


def even_output(o_lat, o_moba, mla_w_uv, w_out0):
    B, T = o_lat.shape[:2]
    o_mla = jnp.einsum('bthc,chv->bthv', o_lat, mla_w_uv).reshape(B, T, MLA_HEADS * MLA_V)
    return jnp.concatenate([o_mla, o_moba.reshape(B, T, MOBA_HEADS * MOBA_HD)], axis=-1) @ w_out0


def mla_prompt(q_lat, q_rope, latent, k_rope):
    B, T, H, C = q_lat.shape
    scale = (MLA_NOPE + MLA_ROPE) ** -0.5
    kpos = jnp.arange(T)

    def block(i):
        t0 = i * MLA_QBLOCK
        ql = lax.dynamic_slice_in_dim(q_lat, t0, MLA_QBLOCK, axis=1)
        qr = lax.dynamic_slice_in_dim(q_rope, t0, MLA_QBLOCK, axis=1)
        s = jnp.einsum('bqhc,bkc->bhqk', ql, latent) + jnp.einsum('bqhr,bkr->bhqk', qr, k_rope)
        s = s.astype(jnp.float32) * scale
        qpos = t0 + jnp.arange(MLA_QBLOCK)
        s = jnp.where((kpos[None, :] <= qpos[:, None])[None, None], s, NEG_INF)
        p = jax.nn.softmax(s, axis=-1).astype(latent.dtype)
        return jnp.einsum('bhqk,bkc->bqhc', p, latent)

    o = lax.map(block, jnp.arange(T // MLA_QBLOCK))
    return jnp.moveaxis(o, 0, 1).reshape(B, T, H, C)


def mla_sample(q_lat, q_rope, latent_new, krope_new, cache_lat, cache_kr, page_table):
    DB, S = q_lat.shape[:2]
    P = page_table.shape[1] * PAGE_SIZE
    scale = (MLA_NOPE + MLA_ROPE) ** -0.5
    mask = jnp.arange(P + S)[None, :] <= (P + jnp.arange(S))[:, None]

    def one(args):
        pt, ql, qr, ln, krn = args
        lat = jnp.concatenate([cache_lat[pt].reshape(P, MLA_KV_LORA), ln], axis=0)
        kr = jnp.concatenate([cache_kr[pt].reshape(P, MLA_ROPE), krn], axis=0)
        s = jnp.einsum('qhc,kc->hqk', ql, lat) + jnp.einsum('qhr,kr->hqk', qr, kr)
        s = jnp.where(mask[None], s.astype(jnp.float32) * scale, NEG_INF)
        p = jax.nn.softmax(s, axis=-1).astype(lat.dtype)
        return jnp.einsum('hqk,kc->qhc', p, lat)

    return lax.map(one, (page_table, q_lat, q_rope, latent_new, krope_new))


def moba_prompt(q, k, v):
    B, T = q.shape[:2]
    KVH, G, HD, BLK = MOBA_KV_HEADS, MOBA_GROUP, MOBA_HD, MOBA_BLOCK
    nb = -(-T // BLK)
    pad = nb * BLK - T
    kp = jnp.pad(k, ((0, 0), (0, pad), (0, 0), (0, 0)))
    vp = jnp.pad(v, ((0, 0), (0, pad), (0, 0), (0, 0)))
    kb = kp.reshape(B, nb, BLK, KVH, HD).transpose(0, 3, 1, 2, 4)
    vb = vp.reshape(B, nb, BLK, KVH, HD).transpose(0, 3, 1, 2, 4)
    kmean = jnp.mean(kb.astype(jnp.float32), axis=3)
    qg = q.reshape(B, T, KVH, G, HD)
    gate = jnp.einsum('btkgd,bknd->btkgn', qg.astype(jnp.float32), kmean)
    n_full = jnp.arange(T) // BLK
    past = jnp.arange(nb)[None, :] < n_full[:, None]
    gate = jnp.where(past[None, :, None, None, :], gate, NEG_INF)
    ksel = min(MOBA_TOPK, nb)
    _, sel = lax.top_k(gate, ksel)
    valid = jnp.arange(ksel)[None, :] < n_full[:, None]
    scale = HD ** -0.5
    bi = jnp.arange(B)[:, None, None, None, None]
    hi = jnp.arange(KVH)[None, None, :, None, None]

    def chunk(ci):
        t0 = ci * MOBA_QCHUNK
        qc = lax.dynamic_slice_in_dim(qg, t0, MOBA_QCHUNK, axis=1)
        sc = lax.dynamic_slice_in_dim(sel, t0, MOBA_QCHUNK, axis=1)
        vc = jnp.repeat(lax.dynamic_slice_in_dim(valid, t0, MOBA_QCHUNK, axis=0), BLK, axis=1)
        k_sel = kb[bi, hi, sc].reshape(B, MOBA_QCHUNK, KVH, G, ksel * BLK, HD)
        v_sel = vb[bi, hi, sc].reshape(B, MOBA_QCHUNK, KVH, G, ksel * BLK, HD)
        s_sel = jnp.einsum('bqkgd,bqkgjd->bqkgj', qc, k_sel).astype(jnp.float32) * scale
        s_sel = jnp.where(vc[None, :, None, None, :], s_sel, NEG_INF)
        j = t0 // BLK
        k_own = lax.dynamic_index_in_dim(kb, j, axis=2, keepdims=False)
        v_own = lax.dynamic_index_in_dim(vb, j, axis=2, keepdims=False)
        s_own = jnp.einsum('bqkgd,bkpd->bqkgp', qc, k_own).astype(jnp.float32) * scale
        qpos = t0 + jnp.arange(MOBA_QCHUNK)
        kpos = j * BLK + jnp.arange(BLK)
        s_own = jnp.where((kpos[None, :] <= qpos[:, None])[None, :, None, None, :], s_own, NEG_INF)
        p = jax.nn.softmax(jnp.concatenate([s_sel, s_own], axis=-1), axis=-1).astype(v.dtype)
        o = jnp.einsum('bqkgj,bqkgjd->bqkgd', p[..., : ksel * BLK], v_sel) + jnp.einsum('bqkgp,bkpd->bqkgd', p[..., ksel * BLK:], v_own)
        return o.reshape(B, MOBA_QCHUNK, MOBA_HEADS, HD)

    o = lax.map(chunk, jnp.arange(T // MOBA_QCHUNK))
    return jnp.moveaxis(o, 0, 1).reshape(B, T, MOBA_HEADS, HD)


def moba_sample(q, k_new, v_new, cache_k, cache_v, page_table):
    DB, S = q.shape[:2]
    KVH, G, HD, BLK = MOBA_KV_HEADS, MOBA_GROUP, MOBA_HD, MOBA_BLOCK
    n_pages = page_table.shape[1]
    P = n_pages * PAGE_SIZE
    ppb = BLK // PAGE_SIZE
    n_full = P // BLK
    r = P - n_full * BLK
    ksel = min(MOBA_TOPK, n_full)
    scale = HD ** -0.5
    qg = q.reshape(DB, S, KVH, G, HD)
    qpos = P + jnp.arange(S)
    kpos_own = n_full * BLK + jnp.arange(r + S)
    own_mask = kpos_own[None, :] <= qpos[:, None]
    kv_idx = jnp.arange(KVH)[None, :, None, None, None]

    def one(args):
        pt, qs, kn, vn = args
        own_pages = pt[n_full * ppb:]
        k_own = jnp.concatenate([cache_k[own_pages].reshape(r, KVH, HD), kn], axis=0)
        v_own = jnp.concatenate([cache_v[own_pages].reshape(r, KVH, HD), vn], axis=0)
        s_own = jnp.einsum('qkgd,pkd->qkgp', qs, k_own).astype(jnp.float32) * scale
        s_own = jnp.where(own_mask[:, None, None, :], s_own, NEG_INF)
        if ksel == 0:
            p = jax.nn.softmax(s_own, axis=-1).astype(vn.dtype)
            o = jnp.einsum('qkgp,pkd->qkgd', p, v_own)
        else:
            k_past = cache_k[pt[: n_full * ppb]].reshape(n_full, BLK, KVH, HD)
            kmean = jnp.mean(k_past.astype(jnp.float32), axis=1)
            gate = jnp.einsum('qkgd,nkd->qkgn', qs.astype(jnp.float32), kmean)
            _, sel = lax.top_k(gate, ksel)
            pages = pt[sel[..., None] * ppb + jnp.arange(ppb)]
            k_sel = cache_k[pages, :, kv_idx].reshape(S, KVH, G, ksel * BLK, HD)
            v_sel = cache_v[pages, :, kv_idx].reshape(S, KVH, G, ksel * BLK, HD)
            s_sel = jnp.einsum('qkgd,qkgjd->qkgj', qs, k_sel).astype(jnp.float32) * scale
            p = jax.nn.softmax(jnp.concatenate([s_sel, s_own], axis=-1), axis=-1).astype(vn.dtype)
            o = jnp.einsum('qkgj,qkgjd->qkgd', p[..., : ksel * BLK], v_sel) + jnp.einsum('qkgp,pkd->qkgd', p[..., ksel * BLK:], v_own)
        return o.reshape(S, MOBA_HEADS, HD)

    return lax.map(one, (page_table, qg, k_new, v_new))


def mlstm_chunkwise(q, k, v, ig, lf, C0, n0, m0):
    B, T, H, DK = q.shape
    DV = v.shape[-1]
    L = MLSTM_CHUNK if T % MLSTM_CHUNK == 0 else T
    nc = T // L

    def chunks(a):
        a = a.reshape((B, nc, L) + a.shape[2:])
        return jnp.moveaxis(a, 1, 0).swapaxes(2, 3)

    causal = jnp.tril(jnp.ones((L, L), dtype=bool))

    def step(carry, inp):
        C, n, m = carry
        qc, kc, vc, igc, lfc = inp
        b = jnp.cumsum(lfc, axis=-1)
        logw = jnp.where(causal, b[..., :, None] - b[..., None, :] + igc[..., None, :], NEG_INF)
        inter = b + m[..., None]
        m_t = jnp.maximum(inter, jnp.max(logw, axis=-1))
        w_inter = jnp.exp(inter - m_t)
        s = jnp.einsum('bhtd,bhsd->bhts', qc, kc) * jnp.exp(logw - m_t[..., None])
        num = w_inter[..., None] * jnp.einsum('bhtd,bhdv->bhtv', qc, C) + jnp.einsum('bhts,bhsv->bhtv', s, vc)
        den = w_inter * jnp.einsum('bhtd,bhd->bht', qc, n) + jnp.sum(s, axis=-1)
        h = num / jnp.maximum(jnp.abs(den), jnp.exp(-m_t))[..., None]
        b_last = b[..., -1]
        logg = b_last[..., None] - b + igc
        m_new = jnp.maximum(b_last + m, jnp.max(logg, axis=-1))
        a = jnp.exp(b_last + m - m_new)
        wg = jnp.exp(logg - m_new[..., None])
        C_new = a[..., None, None] * C + jnp.einsum('bhs,bhsd,bhsv->bhdv', wg, kc, vc)
        n_new = a[..., None] * n + jnp.einsum('bhs,bhsd->bhd', wg, kc)
        return (C_new, n_new, m_new), h

    (C, n, m), hs = lax.scan(step, (C0, n0, m0), (chunks(q), chunks(k), chunks(v), chunks(ig), chunks(lf)))
    h = jnp.moveaxis(hs.swapaxes(2, 3), 0, 1).reshape(B, T, H, DV)
    return h, C, n, m


def mlstm_mixer(h, C0, n0, m0, w_in1, mlstm_gate_bias, mlstm_norm_g, w_out1):
    B, T, _ = h.shape
    H = MLSTM_HEADS
    f32 = jnp.float32
    q, k, v, o, gates = _split(h @ w_in1, IN1_SPLITS)
    q = q.reshape(B, T, H, MLSTM_DK).astype(f32) * (MLSTM_DK ** -0.5)
    k = k.reshape(B, T, H, MLSTM_DK).astype(f32)
    v = v.reshape(B, T, H, MLSTM_DV).astype(f32)
    gates = gates.astype(f32) + mlstm_gate_bias.astype(f32)
    ig = gates[..., :H]
    lf = jax.nn.log_sigmoid(gates[..., H:])
    hh, C, n, m = mlstm_chunkwise(q, k, v, ig, lf, C0.astype(f32), n0.astype(f32), m0.astype(f32))
    hh = hh * lax.rsqrt(jnp.mean(hh * hh, axis=-1, keepdims=True) + RMS_EPS)
    hh = hh.reshape(B, T, H * MLSTM_DV) * mlstm_norm_g.astype(f32) * jax.nn.sigmoid(o.astype(f32))
    return hh.astype(h.dtype) @ w_out1, C, n, m


def moe_swiglu(h, w_router, b_router, w1, w3, w2):
    B, T, D = h.shape
    N = B * T
    x = h.reshape(N, D)
    logits = x.astype(jnp.float32) @ w_router.astype(jnp.float32) + b_router.astype(jnp.float32)
    probs = jax.nn.softmax(logits, axis=-1)
    gate, eidx = lax.top_k(probs, MOE_TOPK)
    gate = gate / jnp.sum(gate, axis=-1, keepdims=True)
    NK = N * MOE_TOPK
    flat_e = eidx.reshape(NK)
    order = jnp.argsort(flat_e)
    e_sorted = flat_e[order]
    tok_sorted = order // MOE_TOPK
    counts = jnp.zeros((N_EXPERTS,), jnp.int32).at[flat_e].add(1)
    starts = jnp.cumsum(counts) - counts
    padded = (counts + MOE_BLOCK - 1) // MOE_BLOCK * MOE_BLOCK
    pends = jnp.cumsum(padded)
    pstarts = pends - padded
    dest = pstarts[e_sorted] + jnp.arange(NK, dtype=jnp.int32) - starts[e_sorted]
    n_blocks = -(-NK // MOE_BLOCK) + N_EXPERTS
    buf = jnp.zeros((n_blocks * MOE_BLOCK, D), x.dtype).at[dest].set(x[tok_sorted])
    block_start = jnp.arange(n_blocks, dtype=jnp.int32) * MOE_BLOCK
    block_e = jnp.minimum(jnp.sum(block_start[:, None] >= pends[None, :], axis=1), N_EXPERTS - 1)

    def expert_block(args):
        xb, e = args
        return (jax.nn.silu(xb @ w1[e]) * (xb @ w3[e])) @ w2[e]

    out = lax.map(expert_block, (buf.reshape(n_blocks, MOE_BLOCK, D), block_e)).reshape(n_blocks * MOE_BLOCK, D)
    w_sorted = gate.reshape(NK)[order].astype(x.dtype)
    y = jax.ops.segment_sum(out[dest] * w_sorted[:, None], tok_sorted, num_segments=N)
    return y.reshape(B, T, D)


def setup_inputs(seed: int = 0) -> dict:
    key = jax.random.key(seed)
    keys = jax.random.split(key, 64)
    ctr = [0]

    def nxt():
        k = keys[ctr[0]]
        ctr[0] += 1
        return k

    def nrm(shape, scale=1.0):
        return jax.random.normal(nxt(), shape, jnp.float32) * scale

    def gain(n):
        return 1.0 + 0.1 * nrm((n,))

    D = D_MODEL
    H = MLSTM_HEADS
    n_pages = PAST_LEN // PAGE_SIZE
    n_pool = (5 * DEC_BATCH * n_pages) // 4
    ada = 0.3 * D ** -0.5
    inp = {}
    inp['x_prompt'] = nrm((BATCH, SEQ, D))
    inp['x_sample'] = nrm((DEC_BATCH, DEC_SEQ, D))
    inp['cache_mla_latent'] = nrm((n_pool, PAGE_SIZE, MLA_KV_LORA))
    inp['cache_mla_krope'] = nrm((n_pool, PAGE_SIZE, MLA_ROPE))
    inp['cache_moba_k'] = nrm((n_pool, PAGE_SIZE, MOBA_KV_HEADS, MOBA_HD))
    inp['cache_moba_v'] = nrm((n_pool, PAGE_SIZE, MOBA_KV_HEADS, MOBA_HD))
    inp['state_mlstm_C'] = nrm((DEC_BATCH, H, MLSTM_DK, MLSTM_DV), 0.5)
    inp['state_mlstm_n'] = nrm((DEC_BATCH, H, MLSTM_DK), 0.5)
    inp['state_mlstm_m'] = nrm((DEC_BATCH, H), 0.5)
    perm = jax.random.permutation(nxt(), n_pool)
    inp['page_table'] = perm[: DEC_BATCH * n_pages].reshape(DEC_BATCH, n_pages).astype(jnp.int32)
    inp['c_prompt'] = nrm((BATCH, D))
    inp['c_sample'] = nrm((DEC_BATCH, D))
    inp['ln0_mix_g'] = gain(D)
    inp['ada0_mix_w'] = nrm((D, 3 * D), ada)
    inp['ada0_mix_b'] = nrm((3 * D,), 0.1)
    inp['w_in0'] = nrm((D, IN0_WIDTH), D ** -0.5)
    inp['mla_q_norm_g'] = gain(MLA_Q_LORA)
    inp['mla_w_uq'] = nrm((MLA_Q_LORA, MLA_HEADS * (MLA_NOPE + MLA_ROPE)), MLA_Q_LORA ** -0.5)
    inp['mla_kv_norm_g'] = gain(MLA_KV_LORA)
    inp['mla_w_uk'] = nrm((MLA_KV_LORA, MLA_HEADS, MLA_NOPE), MLA_KV_LORA ** -0.5)
    inp['mla_w_uv'] = nrm((MLA_KV_LORA, MLA_HEADS, MLA_V), MLA_KV_LORA ** -0.5)
    inp['w_out0'] = nrm((MIX0_OUT, D), MIX0_OUT ** -0.5)
    inp['ln0_ffn_g'] = gain(D)
    inp['ada0_ffn_w'] = nrm((D, 3 * D), ada)
    inp['ada0_ffn_b'] = nrm((3 * D,), 0.1)
    inp['ffn0_w1'] = nrm((D, D_FF), D ** -0.5)
    inp['ffn0_w3'] = nrm((D, D_FF), D ** -0.5)
    inp['ffn0_w2'] = nrm((D_FF, D), D_FF ** -0.5)
    inp['ln1_mix_g'] = gain(D)
    inp['ada1_mix_w'] = nrm((D, 3 * D), ada)
    inp['ada1_mix_b'] = nrm((3 * D,), 0.1)
    inp['w_in1'] = nrm((D, IN1_WIDTH), D ** -0.5)
    inp['mlstm_gate_bias'] = jnp.concatenate([nrm((H,), 0.1), 3.0 + nrm((H,), 0.5)])
    inp['mlstm_norm_g'] = gain(H * MLSTM_DV)
    inp['w_out1'] = nrm((H * MLSTM_DV, D), (H * MLSTM_DV) ** -0.5)
    inp['ln1_ffn_g'] = gain(D)
    inp['ada1_ffn_w'] = nrm((D, 3 * D), ada)
    inp['ada1_ffn_b'] = nrm((3 * D,), 0.1)
    inp['moe_w_router'] = nrm((D, N_EXPERTS), D ** -0.5)
    inp['moe_b_router'] = nrm((N_EXPERTS,), 0.01)
    inp['moe_w1'] = nrm((N_EXPERTS, D, D_FF_EXPERT), D ** -0.5)
    inp['moe_w3'] = nrm((N_EXPERTS, D, D_FF_EXPERT), D ** -0.5)
    inp['moe_w2'] = nrm((N_EXPERTS, D_FF_EXPERT, D), D_FF_EXPERT ** -0.5)
    inp['final_norm_g'] = gain(D)
    return inp


def reference(x_prompt, x_sample, cache_mla_latent, cache_mla_krope, cache_moba_k, cache_moba_v,
              state_mlstm_C, state_mlstm_n, state_mlstm_m, page_table, c_prompt, c_sample,
              ln0_mix_g, ada0_mix_w, ada0_mix_b, w_in0, mla_q_norm_g, mla_w_uq, mla_kv_norm_g,
              mla_w_uk, mla_w_uv, w_out0, ln0_ffn_g, ada0_ffn_w, ada0_ffn_b, ffn0_w1, ffn0_w3, ffn0_w2,
              ln1_mix_g, ada1_mix_w, ada1_mix_b, w_in1, mlstm_gate_bias, mlstm_norm_g, w_out1,
              ln1_ffn_g, ada1_ffn_w, ada1_ffn_b, moe_w_router, moe_b_router, moe_w1, moe_w3, moe_w2,
              final_norm_g):

    def run(x, c, pos, attend_mla, attend_moba, C0, n0, m0):
        for layer in range(DEPTH):
            if layer % 2 == 0:
                h, g = adaln(x, c, ln0_mix_g, ada0_mix_w, ada0_mix_b)
                q_lat, q_rope, latent, k_rope, mq, mk, mv = even_project(h, pos, w_in0, mla_q_norm_g, mla_w_uq, mla_kv_norm_g, mla_w_uk)
                o = even_output(attend_mla(q_lat, q_rope, latent, k_rope), attend_moba(mq, mk, mv), mla_w_uv, w_out0)
                x = x + g * o
                h, g = adaln(x, c, ln0_ffn_g, ada0_ffn_w, ada0_ffn_b)
                x = x + g * swiglu(h, ffn0_w1, ffn0_w3, ffn0_w2)
            else:
                h, g = adaln(x, c, ln1_mix_g, ada1_mix_w, ada1_mix_b)
                o, C, n, m = mlstm_mixer(h, C0, n0, m0, w_in1, mlstm_gate_bias, mlstm_norm_g, w_out1)
                x = x + g * o
                h, g = adaln(x, c, ln1_ffn_g, ada1_ffn_w, ada1_ffn_b)
                x = x + g * moe_swiglu(h, moe_w_router, moe_b_router, moe_w1, moe_w3, moe_w2)
        return rmsnorm(x, final_norm_g), latent, k_rope, mk, mv, C, n, m

    def attend_mla_sample(ql, qr, lat, kr):
        return mla_sample(ql, qr, lat, kr, cache_mla_latent, cache_mla_krope, page_table)

    def attend_moba_sample(mq, mk, mv):
        return moba_sample(mq, mk, mv, cache_moba_k, cache_moba_v, page_table)

    Bp, T = x_prompt.shape[:2]
    S = x_sample.shape[1]
    P = page_table.shape[1] * PAGE_SIZE
    pos_p = jnp.arange(T).astype(jnp.float32)
    pos_s = (P + jnp.arange(S)).astype(jnp.float32)
    C0p = jnp.zeros((Bp, MLSTM_HEADS, MLSTM_DK, MLSTM_DV), jnp.float32)
    n0p = jnp.zeros((Bp, MLSTM_HEADS, MLSTM_DK), jnp.float32)
    m0p = jnp.zeros((Bp, MLSTM_HEADS), jnp.float32)

    y_prompt, p_lat, p_kr, p_k, p_v, p_C, p_n, p_m = run(x_prompt, c_prompt, pos_p, mla_prompt, moba_prompt, C0p, n0p, m0p)
    y_sample, s_lat, s_kr, s_k, s_v, s_C, s_n, s_m = run(x_sample, c_sample, pos_s, attend_mla_sample, attend_moba_sample, state_mlstm_C, state_mlstm_n, state_mlstm_m)
    return (y_prompt, y_sample, p_lat, p_kr, p_k, p_v, p_C, p_n, p_m, s_lat, s_kr, s_k, s_v, s_C, s_n, s_m)
```

```python
import functools

import jax
import jax.numpy as jnp
from jax import lax
from jax.experimental import pallas as pl
from jax.experimental.pallas import tpu as pltpu

F32 = jnp.float32
BF16 = jnp.bfloat16

ROPE_THETA = 10000.0
RMS_EPS = 1e-6
NEG_INF = -1e30
PAGE_SIZE = 128

MLA_HEADS = 8
MLA_Q_LORA = 256
MLA_KV_LORA = 128
MLA_NOPE = 64
MLA_ROPE = 32
MLA_V = 64
MLA_SCALE = (MLA_NOPE + MLA_ROPE) ** -0.5
MLA_KW = 256

MOBA_HEADS = 8
MOBA_KV_HEADS = 4
MOBA_HD = 64
MOBA_BLOCK = 256
MOBA_TOPK = 3
MOBA_SCALE = MOBA_HD ** -0.5

MLSTM_HEADS = 8
MLSTM_DK = 64
MLSTM_DV = 128

N_EXPERTS = 8
LANES = 128
VMEM_LIMIT = 56 * 1024 * 1024


def _cparams(sem, vmem=VMEM_LIMIT):
    return pltpu.CompilerParams(dimension_semantics=sem, vmem_limit_bytes=vmem)


def _dot(a, b):
    return jnp.dot(a, b, preferred_element_type=F32)


def _dot_nt(a, b):
    return lax.dot_general(a, b, (((1,), (1,)), ((), ())), preferred_element_type=F32)


def _dot_tn(a, b):
    return lax.dot_general(a, b, (((0,), (0,)), ((), ())), preferred_element_type=F32)


def _split_bf16(a):
    hi = a.astype(BF16)
    lo = (a - hi.astype(F32)).astype(BF16)
    return hi, lo


def _dot3(a, b):
    ah, al = _split_bf16(a)
    bh, bl = _split_bf16(b)
    return _dot(ah, bh) + (_dot(ah, bl) + _dot(al, bh))


def _rms(x):
    return x * lax.rsqrt(jnp.mean(x * x, axis=-1, keepdims=True) + RMS_EPS)


def _adaln(x, g, scale, shift):
    return _rms(x) * g * (1.0 + scale) + shift


def _rope(x, tab_ref, half):
    return (x * tab_ref[0] + pltpu.roll(x, LANES - half, 1) * tab_ref[1]
            + pltpu.roll(x, half, 1) * tab_ref[2])


def _sigmoid(x):
    return 1.0 / (1.0 + jnp.exp(-x))


def _mod_body(c_ref, w_ref, b_ref, o_ref):
    o_ref[...] = _dot3(c_ref[...], w_ref[...]) + b_ref[...]


def _modulation(c, w, b):
    rows, d = c.shape
    n3 = w.shape[1]
    tn = d
    return pl.pallas_call(
        _mod_body,
        grid=(n3 // tn,),
        in_specs=[pl.BlockSpec((rows, d), lambda j: (0, 0)),
                  pl.BlockSpec((d, tn), lambda j: (0, j)),
                  pl.BlockSpec((1, tn), lambda j: (0, j))],
        out_specs=pl.BlockSpec((rows, tn), lambda j: (0, j)),
        out_shape=jax.ShapeDtypeStruct((rows, n3), F32),
        compiler_params=_cparams(("arbitrary",)),
        name="adaln_mod",
    )(c, w, b.reshape(1, n3))


def _foldq_body(wn_ref, wr_ref, wk_ref, o_ref):
    ql = _dot3(wn_ref[0], wk_ref[0])
    o_ref[0] = jnp.concatenate([ql, wr_ref[0]], axis=1).astype(BF16)


def _fold_q_weights(mla_w_uq, mla_w_uk):
    hq = MLA_NOPE + MLA_ROPE
    wq = mla_w_uq.reshape(MLA_Q_LORA, MLA_HEADS, hq).transpose(1, 0, 2)
    wn = wq[:, :, :MLA_NOPE]
    wr = jnp.pad(wq[:, :, MLA_NOPE:], ((0, 0), (0, 0), (0, LANES - MLA_ROPE)))
    wk = mla_w_uk.transpose(1, 2, 0)
    return pl.pallas_call(
        _foldq_body,
        grid=(MLA_HEADS,),
        in_specs=[pl.BlockSpec((1, MLA_Q_LORA, MLA_NOPE), lambda h: (h, 0, 0)),
                  pl.BlockSpec((1, MLA_Q_LORA, LANES), lambda h: (h, 0, 0)),
                  pl.BlockSpec((1, MLA_NOPE, MLA_KV_LORA), lambda h: (h, 0, 0))],
        out_specs=pl.BlockSpec((1, MLA_Q_LORA, MLA_KW), lambda h: (h, 0, 0)),
        out_shape=jax.ShapeDtypeStruct((MLA_HEADS, MLA_Q_LORA, MLA_KW), BF16),
        compiler_params=_cparams(("arbitrary",)),
        name="mla_fold_q",
    )(wn, wr, wk)


def _proj0_body(emit_kmean, x_ref, sh_ref, sc_ref, g_ref, w0_ref, wq_ref, qg_ref, kvg_ref,
                t32_ref, t64_ref, qf_ref, kf_ref, lat_ref, kr_ref, mq_ref, mk_ref, mv_ref,
                mkb_ref, mvb_ref, *km_ref):
    h = _adaln(x_ref[...], g_ref[...], sc_ref[0], sh_ref[0])
    z = _dot(h.astype(BF16), w0_ref[...])
    qn = (_rms(z[:, 0:256]) * qg_ref[...]).astype(BF16)
    lat = _rms(z[:, 256:384]) * kvg_ref[...]
    kr = _rope(z[:, 384:512], t32_ref, MLA_ROPE // 2)
    lat_ref[...] = lat
    kr_ref[...] = kr[:, :MLA_ROPE]
    kf_ref[...] = jnp.concatenate([lat, kr], axis=1).astype(BF16)
    for hd in range(MLA_HEADS):
        qh = _dot(qn, wq_ref[hd])
        qr = _rope(qh[:, LANES:], t32_ref, MLA_ROPE // 2)
        qf_ref[hd] = (jnp.concatenate([qh[:, :LANES], qr], axis=1) * MLA_SCALE).astype(BF16)
    mq = [_rope(z[:, 512 + c * LANES: 640 + c * LANES], t64_ref, MOBA_HD // 2) for c in range(4)]
    mq_ref[...] = (jnp.concatenate(mq, axis=1) * MOBA_SCALE).astype(BF16)
    mk = [_rope(z[:, 1024 + c * LANES: 1152 + c * LANES], t64_ref, MOBA_HD // 2) for c in range(2)]
    mk = jnp.concatenate(mk, axis=1)
    mv = z[:, 1280:1536]
    mk_ref[...] = mk
    mv_ref[...] = mv
    mkb_ref[...] = mk.astype(BF16)
    mvb_ref[...] = mv.astype(BF16)
    if emit_kmean:
        for j in range(mk.shape[0] // MOBA_BLOCK):
            km_ref[0][j] = jnp.mean(mk[j * MOBA_BLOCK:(j + 1) * MOBA_BLOCK], axis=0, keepdims=True)


def _proj0(x, mods, tpg, tm, g, w0, wq, qg, kvg, t32, t64, emit_kmean):
    n, d = x.shape
    nt = n // tm
    sh, sc = mods
    mrows = sh.shape[1]
    tab_map = (lambda i: (0, i % tpg, 0)) if emit_kmean else (lambda i: (0, 0, 0))
    tok = lambda w: pl.BlockSpec((tm, w), lambda i: (i, 0))
    const2 = lambda a: pl.BlockSpec(a.shape, lambda i: (0, 0))
    const3 = lambda a: pl.BlockSpec(a.shape, lambda i: (0, 0, 0))
    mod_spec = pl.BlockSpec((1, mrows, d), lambda i: (i // tpg, 0, 0))
    out_shape = [jax.ShapeDtypeStruct((MLA_HEADS, n, MLA_KW), BF16),
                 jax.ShapeDtypeStruct((n, MLA_KW), BF16),
                 jax.ShapeDtypeStruct((n, MLA_KV_LORA), F32),
                 jax.ShapeDtypeStruct((n, MLA_ROPE), F32),
                 jax.ShapeDtypeStruct((n, 512), BF16),
                 jax.ShapeDtypeStruct((n, 256), F32),
                 jax.ShapeDtypeStruct((n, 256), F32),
                 jax.ShapeDtypeStruct((n, 256), BF16),
                 jax.ShapeDtypeStruct((n, 256), BF16)]
    out_specs = [pl.BlockSpec((MLA_HEADS, tm, MLA_KW), lambda i: (0, i, 0)),
                 tok(MLA_KW), tok(MLA_KV_LORA), tok(MLA_ROPE), tok(512), tok(256), tok(256),
                 tok(256), tok(256)]
    if emit_kmean:
        nbt = tm // MOBA_BLOCK
        out_shape.append(jax.ShapeDtypeStruct((n // MOBA_BLOCK, 1, 256), F32))
        out_specs.append(pl.BlockSpec((nbt, 1, 256), lambda i: (i, 0, 0)))
    return pl.pallas_call(
        functools.partial(_proj0_body, emit_kmean),
        grid=(nt,),
        in_specs=[tok(d), mod_spec, mod_spec, const2(g), const2(w0), const3(wq), const2(qg),
                  const2(kvg),
                  pl.BlockSpec((3, tm, LANES), tab_map), pl.BlockSpec((3, tm, LANES), tab_map)],
        out_specs=out_specs,
        out_shape=out_shape,
        compiler_params=_cparams(("arbitrary",)),
        name="proj0",
    )(x, sh, sc, g, w0, wq, qg, kvg, t32, t64)


def _mla_prompt_body(tq, q_ref, k_ref, wuv_ref, o_ref, m_ref, l_ref, acc_ref):
    i = pl.program_id(1)
    rows = MLA_HEADS * tq
    q = q_ref[...].reshape(rows, MLA_KW)
    m_ref[...] = jnp.full((rows, 1), NEG_INF, F32)
    l_ref[...] = jnp.zeros((rows, 1), F32)
    acc_ref[...] = jnp.zeros((rows, MLA_KV_LORA), F32)

    def step(j, masked):
        k = k_ref[pl.ds(pl.multiple_of(j * tq, tq), tq), :]
        s = _dot_nt(q, k)
        if masked:
            qpos = lax.broadcasted_iota(jnp.int32, (rows, tq), 0) % tq
            kpos = lax.broadcasted_iota(jnp.int32, (rows, tq), 1)
            s = jnp.where(kpos <= qpos, s, NEG_INF)
        m_prev = m_ref[...]
        m_new = jnp.maximum(m_prev, jnp.max(s, axis=-1, keepdims=True))
        alpha = jnp.exp(m_prev - m_new)
        p = jnp.exp(s - m_new)
        l_ref[...] = alpha * l_ref[...] + jnp.sum(p, axis=-1, keepdims=True)
        acc_ref[...] = alpha * acc_ref[...] + _dot(p.astype(BF16), k[:, :MLA_KV_LORA])
        m_ref[...] = m_new

    def past(j, carry):
        step(j, False)
        return carry

    lax.fori_loop(0, i, past, 0)
    step(i, True)
    o = (acc_ref[...] / l_ref[...]).astype(BF16)
    outs = [_dot(o[hd * tq:(hd + 1) * tq], wuv_ref[hd]) for hd in range(MLA_HEADS)]
    o_ref[...] = jnp.concatenate(outs, axis=1).astype(o_ref.dtype)


def _mla_prompt(qf, kf, wuv, nb, t, tq):
    n = kf.shape[0]
    nq = t // tq
    rows = MLA_HEADS * tq
    return pl.pallas_call(
        functools.partial(_mla_prompt_body, tq),
        grid=(nb, nq),
        in_specs=[pl.BlockSpec((MLA_HEADS, tq, MLA_KW), lambda b, i: (0, b * nq + i, 0)),
                  pl.BlockSpec((t, MLA_KW), lambda b, i: (b, 0)),
                  pl.BlockSpec(wuv.shape, lambda b, i: (0, 0, 0))],
        out_specs=pl.BlockSpec((tq, MLA_HEADS * MLA_V), lambda b, i: (b * nq + i, 0)),
        out_shape=jax.ShapeDtypeStruct((n, MLA_HEADS * MLA_V), BF16),
        scratch_shapes=[pltpu.VMEM((rows, 1), F32), pltpu.VMEM((rows, 1), F32),
                        pltpu.VMEM((rows, MLA_KV_LORA), F32)],
        compiler_params=_cparams(("arbitrary", "arbitrary")),
        name="mla_prompt",
    )(qf, kf, wuv)


def _mla_sample_body(n_pages, ck, pt_ref, q_ref, kn_ref, wuv_ref, lat_hbm, kr_hbm, o_ref,
                     latbuf, krbuf, sem):
    b = pl.program_id(0)
    nreq = pl.num_programs(0)
    slot = b % 2
    s_new = kn_ref.shape[1]

    def lat_copy(page, p, sl):
        return pltpu.make_async_copy(lat_hbm.at[page],
                                     latbuf.at[sl, pl.ds(p * PAGE_SIZE, PAGE_SIZE)], sem.at[0, sl])

    def kr_copy(page, p, sl):
        return pltpu.make_async_copy(kr_hbm.at[page],
                                     krbuf.at[sl, pl.ds(p * PAGE_SIZE, PAGE_SIZE)], sem.at[1, sl])

    def issue(req, sl):
        def body(p, carry):
            page = pt_ref[req, p]
            lat_copy(page, p, sl).start()
            kr_copy(page, p, sl).start()
            return carry
        lax.fori_loop(0, n_pages, body, 0)

    @pl.when(b == 0)
    def _():
        issue(0, 0)

    @pl.when(b + 1 < nreq)
    def _():
        issue(b + 1, 1 - slot)

    def wait_body(p, carry):
        lat_copy(0, p, slot).wait()
        kr_copy(0, p, slot).wait()
        return carry
    lax.fori_loop(0, n_pages, wait_body, 0)

    q = q_ref[0]
    rows = q.shape[0]
    ql = q[:, :MLA_KV_LORA]
    qr = q[:, MLA_KV_LORA:MLA_KV_LORA + MLA_ROPE]

    def chunk(c, carry):
        m_prev, l_prev, acc = carry
        off = pl.multiple_of(c * ck, ck)
        lat = latbuf[slot, pl.ds(off, ck), :].astype(BF16)
        kr = krbuf[slot, pl.ds(off, ck), :].astype(BF16)
        s = _dot_nt(ql, lat) + _dot_nt(qr, kr)
        m_new = jnp.maximum(m_prev, jnp.max(s, axis=-1, keepdims=True))
        alpha = jnp.exp(m_prev - m_new)
        p = jnp.exp(s - m_new)
        l_new = alpha * l_prev + jnp.sum(p, axis=-1, keepdims=True)
        acc = alpha * acc + _dot(p.astype(BF16), lat)
        return m_new, l_new, acc

    init = (jnp.full((rows, 1), NEG_INF, F32), jnp.zeros((rows, 1), F32),
            jnp.zeros((rows, MLA_KV_LORA), F32))
    m_prev, l_prev, acc = lax.fori_loop(0, n_pages * PAGE_SIZE // ck, chunk, init)

    kn = kn_ref[0]
    s = _dot_nt(q.astype(F32), kn)
    qtok = lax.broadcasted_iota(jnp.int32, (rows, s_new), 0) % s_new
    ktok = lax.broadcasted_iota(jnp.int32, (rows, s_new), 1)
    s = jnp.where(ktok <= qtok, s, NEG_INF)
    m_new = jnp.maximum(m_prev, jnp.max(s, axis=-1, keepdims=True))
    alpha = jnp.exp(m_prev - m_new)
    p = jnp.exp(s - m_new)
    l_new = alpha * l_prev + jnp.sum(p, axis=-1, keepdims=True)
    acc = alpha * acc + _dot(p, kn[:, :MLA_KV_LORA])
    o = (acc / l_new).astype(BF16)
    outs = [_dot(o[hd * s_new:(hd + 1) * s_new], wuv_ref[hd]) for hd in range(MLA_HEADS)]
    o_ref[0] = jnp.concatenate(outs, axis=1)


def _mla_sample(page_table, q, kn, wuv, cache_lat, cache_kr):
    nreq, n_pages = page_table.shape
    rows = q.shape[1]
    s_new = kn.shape[1]
    p_len = n_pages * PAGE_SIZE
    ck = min(1024, p_len)
    grid_spec = pltpu.PrefetchScalarGridSpec(
        num_scalar_prefetch=1,
        grid=(nreq,),
        in_specs=[pl.BlockSpec((1, rows, MLA_KW), lambda b, pt: (b, 0, 0)),
                  pl.BlockSpec((1, s_new, MLA_KW), lambda b, pt: (b, 0, 0)),
                  pl.BlockSpec(wuv.shape, lambda b, pt: (0, 0, 0)),
                  pl.BlockSpec(memory_space=pl.ANY),
                  pl.BlockSpec(memory_space=pl.ANY)],
        out_specs=pl.BlockSpec((1, s_new, MLA_HEADS * MLA_V), lambda b, pt: (b, 0, 0)),
        scratch_shapes=[pltpu.VMEM((2, p_len, MLA_KV_LORA), F32),
                        pltpu.VMEM((2, p_len, MLA_ROPE), F32),
                        pltpu.SemaphoreType.DMA((2, 2))])
    return pl.pallas_call(
        functools.partial(_mla_sample_body, n_pages, ck),
        grid_spec=grid_spec,
        out_shape=jax.ShapeDtypeStruct((nreq, s_new, MLA_HEADS * MLA_V), F32),
        compiler_params=_cparams(("arbitrary",)),
        name="mla_sample",
    )(page_table, q, kn, wuv, cache_lat, cache_kr)


def _top_blocks(gate, n_valid, ksel, axis_iota):
    width = gate.shape[1]
    g = jnp.where(axis_iota < n_valid, gate, -jnp.inf)
    sel = jnp.zeros(gate.shape, F32)
    for _ in range(ksel):
        mx = jnp.max(g, axis=-1, keepdims=True)
        idx = jnp.min(jnp.where(g == mx, axis_iota, width), axis=-1, keepdims=True)
        hit = (axis_iota == idx) & (mx > -jnp.inf)
        sel = jnp.where(hit, 1.0, sel)
        g = jnp.where(hit, -jnp.inf, g)
    return sel


def _moba_prompt_body(nblk, q_ref, k_ref, v_ref, km_ref, o_ref, m_ref, l_ref, acc_ref):
    kvh = pl.program_id(1)
    i = pl.program_id(2)
    blk = MOBA_BLOCK
    rows = 2 * blk
    even = (kvh % 2) == 0
    qcol = q_ref[...]
    qswp = jnp.concatenate([qcol[:, MOBA_HD:], qcol[:, :MOBA_HD]], axis=1)
    lane = lax.broadcasted_iota(jnp.int32, (blk, LANES), 1)
    lo = lane < MOBA_HD
    zero = jnp.zeros_like(qcol)
    q0 = jnp.where(even, jnp.where(lo, qcol, zero), jnp.where(lo, zero, qswp))
    q1 = jnp.where(even, jnp.where(lo, qswp, zero), jnp.where(lo, zero, qcol))
    q = jnp.concatenate([q0, q1], axis=0)

    km_hi, km_lo = _split_bf16(km_ref[...])
    gate = _dot_nt(q, km_hi) + _dot_nt(q, km_lo)
    blk_iota = lax.broadcasted_iota(jnp.int32, (rows, nblk), 1)
    sel = _top_blocks(gate, i, min(MOBA_TOPK, nblk), blk_iota).astype(BF16)

    def scores(j):
        off = pl.multiple_of(j * blk, blk)
        return _dot_nt(q, k_ref[pl.ds(off, blk), :]), v_ref[pl.ds(off, blk), :]

    s, v = scores(i)
    qpos = lax.broadcasted_iota(jnp.int32, (rows, blk), 0) % blk
    kpos = lax.broadcasted_iota(jnp.int32, (rows, blk), 1)
    s = jnp.where(kpos <= qpos, s, NEG_INF)
    m0 = jnp.max(s, axis=-1, keepdims=True)
    p = jnp.exp(s - m0)
    m_ref[...] = m0
    l_ref[...] = jnp.sum(p, axis=-1, keepdims=True)
    acc_ref[...] = _dot(p.astype(BF16), v)

    def past(j, carry):
        s, v = scores(j)
        onehot = (lax.broadcasted_iota(jnp.int32, (nblk, LANES), 0) == j).astype(BF16)
        chosen = _dot(sel, onehot) > 0.5
        s = jnp.where(jnp.concatenate([chosen, chosen], axis=1), s, NEG_INF)
        m_prev = m_ref[...]
        m_new = jnp.maximum(m_prev, jnp.max(s, axis=-1, keepdims=True))
        alpha = jnp.exp(m_prev - m_new)
        p = jnp.exp(s - m_new)
        l_ref[...] = alpha * l_ref[...] + jnp.sum(p, axis=-1, keepdims=True)
        acc_ref[...] = alpha * acc_ref[...] + _dot(p.astype(BF16), v)
        m_ref[...] = m_new
        return carry

    lax.fori_loop(0, i, past, 0)
    o = acc_ref[...] / l_ref[...]
    o = jnp.where(even, o[:, :MOBA_HD], o[:, MOBA_HD:])
    o_ref[...] = jnp.concatenate([o[:blk], o[blk:]], axis=1).astype(o_ref.dtype)


def _moba_prompt(mq, mkb, mvb, kmean, nb, t):
    n = mq.shape[0]
    blk = MOBA_BLOCK
    nblk = t // blk
    rows = 2 * blk
    return pl.pallas_call(
        functools.partial(_moba_prompt_body, nblk),
        grid=(nb, MOBA_KV_HEADS, nblk),
        in_specs=[pl.BlockSpec((blk, LANES), lambda b, h, i: (b * nblk + i, h)),
                  pl.BlockSpec((t, LANES), lambda b, h, i: (b, h // 2)),
                  pl.BlockSpec((t, LANES), lambda b, h, i: (b, h // 2)),
                  pl.BlockSpec((nblk, LANES), lambda b, h, i: (b, h // 2))],
        out_specs=pl.BlockSpec((blk, LANES), lambda b, h, i: (b * nblk + i, h)),
        out_shape=jax.ShapeDtypeStruct((n, MOBA_HEADS * MOBA_HD), BF16),
        scratch_shapes=[pltpu.VMEM((rows, 1), F32), pltpu.VMEM((rows, 1), F32),
                        pltpu.VMEM((rows, LANES), F32)],
        compiler_params=_cparams(("arbitrary", "arbitrary", "arbitrary")),
        name="moba_prompt",
    )(mq, mkb, mvb, kmean)


def _moba_sample_body(n_pages, cp, q_ref_pt, q_ref, kn_ref, vn_ref, k_hbm, v_hbm, o_ref,
                      kbuf, vbuf, g_ref, mx_ref, ls_ref, op_ref, sem):
    pt_ref = q_ref_pt
    b = pl.program_id(0)
    c = pl.program_id(1)
    nreq = pl.num_programs(0)
    nch = pl.num_programs(1)
    step = b * nch + c
    slot = step % 2
    blk = MOBA_BLOCK
    bpc = cp * PAGE_SIZE // blk
    nblk = n_pages * PAGE_SIZE // blk
    s_new = kn_ref.shape[1]
    prow = q_ref.shape[2]

    def k_copy(page, p, sl):
        return pltpu.make_async_copy(k_hbm.at[page],
                                     kbuf.at[sl, pl.ds(p * PAGE_SIZE, PAGE_SIZE)], sem.at[0, sl])

    def v_copy(page, p, sl):
        return pltpu.make_async_copy(v_hbm.at[page],
                                     vbuf.at[sl, pl.ds(p * PAGE_SIZE, PAGE_SIZE)], sem.at[1, sl])

    def issue(req, chunk, sl):
        def body(p, carry):
            page = pt_ref[req, chunk * cp + p]
            k_copy(page, p, sl).start()
            v_copy(page, p, sl).start()
            return carry
        lax.fori_loop(0, cp, body, 0)

    @pl.when(step == 0)
    def _():
        issue(0, 0, 0)

    @pl.when(step + 1 < nreq * nch)
    def _():
        nxt = step + 1
        issue(nxt // nch, nxt % nch, 1 - slot)

    def wait_body(p, carry):
        k_copy(0, p, slot).wait()
        v_copy(0, p, slot).wait()
        return carry
    lax.fori_loop(0, cp, wait_body, 0)

    @pl.when(c == 0)
    def _():
        g_ref[...] = jnp.zeros(g_ref.shape, F32)
        mx_ref[...] = jnp.zeros(mx_ref.shape, F32)
        ls_ref[...] = jnp.zeros(ls_ref.shape, F32)

    lane = lax.broadcasted_iota(jnp.int32, (prow, LANES), 1)

    def block(j, carry):
        off = pl.multiple_of(j * blk, blk)
        kj = kbuf[slot, pl.ds(off, blk), :]
        vj = vbuf[slot, pl.ds(off, blk), :]
        kmean = jnp.mean(kj, axis=0, keepdims=True)
        kjb = kj.astype(BF16)
        vjb = vj.astype(BF16)
        jg = c * bpc + j
        hit = lane == jg
        for pr in range(2):
            qp = q_ref[0, pr]
            cols = slice(pr * LANES, (pr + 1) * LANES)
            s = _dot_nt(qp, kjb[:, cols])
            mx = jnp.max(s, axis=-1, keepdims=True)
            p = jnp.exp(s - mx)
            g = jnp.sum(qp.astype(F32) * kmean[:, cols], axis=-1, keepdims=True)
            g_ref[pr] = jnp.where(hit, g, g_ref[pr])
            mx_ref[pr] = jnp.where(hit, mx, mx_ref[pr])
            ls_ref[pr] = jnp.where(hit, jnp.sum(p, axis=-1, keepdims=True), ls_ref[pr])
            op_ref[pr, jg] = _dot(p.astype(BF16), vjb[:, cols])
        return carry

    lax.fori_loop(0, bpc, block, 0)

    @pl.when(c == nch - 1)
    def _():
        kn = kn_ref[0]
        vn = vn_ref[0]
        pieces = {}
        for pr in range(2):
            cols = slice(pr * LANES, (pr + 1) * LANES)
            qp = q_ref[0, pr].astype(F32)
            sel = _top_blocks(g_ref[pr], nblk, min(MOBA_TOPK, nblk), lane) > 0.5
            s = _dot_nt(qp, kn[:, cols])
            qtok = lax.broadcasted_iota(jnp.int32, (prow, s_new), 0) % s_new
            ktok = lax.broadcasted_iota(jnp.int32, (prow, s_new), 1)
            s = jnp.where(ktok <= qtok, s, NEG_INF)
            mx = mx_ref[pr]
            m_tot = jnp.maximum(jnp.max(s, axis=-1, keepdims=True),
                                jnp.max(jnp.where(sel, mx, -jnp.inf), axis=-1, keepdims=True))
            w = jnp.where(sel, jnp.exp(jnp.where(sel, mx, m_tot) - m_tot), 0.0)
            p = jnp.exp(s - m_tot)
            l = (jnp.sum(w * ls_ref[pr], axis=-1, keepdims=True)
                 + jnp.sum(p, axis=-1, keepdims=True))
            o = _dot(p, vn[:, cols])
            for j in range(nblk):
                o = o + w[:, j:j + 1] * op_ref[pr, j]
            o = o / l
            for kk in range(2):
                for g in range(2):
                    r0 = (kk * 2 + g) * s_new
                    head = (2 * pr + kk) * 2 + g
                    pieces[head] = o[r0:r0 + s_new, kk * MOBA_HD:(kk + 1) * MOBA_HD]
        o_ref[0] = jnp.concatenate([pieces[h] for h in range(MOBA_HEADS)], axis=1)


def _moba_sample(page_table, q, kn, vn, cache_k, cache_v):
    nreq, n_pages = page_table.shape
    s_new = kn.shape[1]
    prow = q.shape[2]
    cp = min(16, n_pages)
    assert n_pages % cp == 0 and (cp * PAGE_SIZE) % MOBA_BLOCK == 0
    assert (n_pages * PAGE_SIZE) % MOBA_BLOCK == 0
    nblk = n_pages * PAGE_SIZE // MOBA_BLOCK
    assert nblk <= LANES
    grid_spec = pltpu.PrefetchScalarGridSpec(
        num_scalar_prefetch=1,
        grid=(nreq, n_pages // cp),
        in_specs=[pl.BlockSpec((1, 2, prow, LANES), lambda b, c, pt: (b, 0, 0, 0)),
                  pl.BlockSpec((1, s_new, 256), lambda b, c, pt: (b, 0, 0)),
                  pl.BlockSpec((1, s_new, 256), lambda b, c, pt: (b, 0, 0)),
                  pl.BlockSpec(memory_space=pl.ANY),
                  pl.BlockSpec(memory_space=pl.ANY)],
        out_specs=pl.BlockSpec((1, s_new, MOBA_HEADS * MOBA_HD), lambda b, c, pt: (b, 0, 0)),
        scratch_shapes=[pltpu.VMEM((2, cp * PAGE_SIZE, 256), F32),
                        pltpu.VMEM((2, cp * PAGE_SIZE, 256), F32),
                        pltpu.VMEM((2, prow, LANES), F32),
                        pltpu.VMEM((2, prow, LANES), F32),
                        pltpu.VMEM((2, prow, LANES), F32),
                        pltpu.VMEM((2, nblk, prow, LANES), F32),
                        pltpu.SemaphoreType.DMA((2, 2))])
    return pl.pallas_call(
        functools.partial(_moba_sample_body, n_pages, cp),
        grid_spec=grid_spec,
        out_shape=jax.ShapeDtypeStruct((nreq, s_new, MOBA_HEADS * MOBA_HD), F32),
        compiler_params=_cparams(("arbitrary", "arbitrary")),
        name="moba_sample",
    )(page_table, q, kn, vn, cache_k, cache_v)


def _ffn0_body(nf, x_ref, oa_ref, ob_ref, gm_ref, sh_ref, sc_ref, gf_ref, g_ref, woa_ref, wob_ref,
               w1_ref, w3_ref, w2_ref, y_ref, x1_ref, h_ref, acc_ref):
    f = pl.program_id(1)

    @pl.when(f == 0)
    def _():
        o = _dot(oa_ref[...].astype(BF16), woa_ref[...]) + _dot(ob_ref[...].astype(BF16), wob_ref[...])
        x1 = x_ref[...] + gm_ref[0] * o
        x1_ref[...] = x1
        h_ref[...] = _adaln(x1, g_ref[...], sc_ref[0], sh_ref[0]).astype(BF16)

    h = h_ref[...]
    a = _dot(h, w1_ref[...])
    u = _dot(h, w3_ref[...])
    part = _dot((a * _sigmoid(a) * u).astype(BF16), w2_ref[...])

    @pl.when(f == 0)
    def _():
        acc_ref[...] = part

    @pl.when(f > 0)
    def _():
        acc_ref[...] += part

    @pl.when(f == nf - 1)
    def _():
        y_ref[...] = x1_ref[...] + gf_ref[0] * acc_ref[...]


def _ffn0(x, oa, ob, mods, tpg, tm, g, woa, wob, w1, w3, w2, tf):
    n, d = x.shape
    dff = w1.shape[1]
    nf = dff // tf
    gm, sh, sc, gf = mods
    mrows = gm.shape[1]
    tok = lambda w: pl.BlockSpec((tm, w), lambda i, f: (i, 0))
    mod_spec = pl.BlockSpec((1, mrows, d), lambda i, f: (i // tpg, 0, 0))
    const2 = lambda a: pl.BlockSpec(a.shape, lambda i, f: (0, 0))
    return pl.pallas_call(
        functools.partial(_ffn0_body, nf),
        grid=(n // tm, nf),
        in_specs=[tok(d), tok(oa.shape[1]), tok(ob.shape[1]), mod_spec, mod_spec, mod_spec, mod_spec,
                  const2(g), const2(woa), const2(wob),
                  pl.BlockSpec((d, tf), lambda i, f: (0, f)),
                  pl.BlockSpec((d, tf), lambda i, f: (0, f)),
                  pl.BlockSpec((tf, d), lambda i, f: (f, 0))],
        out_specs=tok(d),
        out_shape=jax.ShapeDtypeStruct((n, d), F32),
        scratch_shapes=[pltpu.VMEM((tm, d), F32), pltpu.VMEM((tm, d), BF16), pltpu.VMEM((tm, d), F32)],
        compiler_params=_cparams(("arbitrary", "arbitrary")),
        name="out0_ffn0",
    )(x, oa, ob, gm, sh, sc, gf, g, woa, wob, w1, w3, w2)


def _proj1_body(x_ref, sh_ref, sc_ref, g_ref, w_ref, wg_ref, gb_ref, q_ref, k_ref, v_ref, so_ref,
                gt_ref):
    h = _adaln(x_ref[...], g_ref[...], sc_ref[0], sh_ref[0])
    z = _dot(h.astype(BF16), w_ref[...])
    hk = MLSTM_HEADS * MLSTM_DK
    hv = MLSTM_HEADS * MLSTM_DV
    q_ref[...] = (z[:, :hk] * (MLSTM_DK ** -0.5)).astype(q_ref.dtype)
    k_ref[...] = z[:, hk:2 * hk].astype(k_ref.dtype)
    v_ref[...] = z[:, 2 * hk:2 * hk + hv].astype(v_ref.dtype)
    so_ref[...] = _sigmoid(z[:, 2 * hk + hv:]).astype(so_ref.dtype)
    gates = _dot3(h, wg_ref[...]) + gb_ref[...]
    logsig = jnp.minimum(gates, 0.0) - jnp.log(1.0 + jnp.exp(-jnp.abs(gates)))
    lane = lax.broadcasted_iota(jnp.int32, gates.shape, 1)
    gt_ref[...] = jnp.where(lane < MLSTM_HEADS, gates, logsig)[:, :2 * MLSTM_HEADS]


def _proj1(x, mods, tpg, tm, g, w, wg, gb, act_dtype):
    n, d = x.shape
    sh, sc = mods
    mrows = sh.shape[1]
    hk = MLSTM_HEADS * MLSTM_DK
    hv = MLSTM_HEADS * MLSTM_DV
    tok = lambda wd: pl.BlockSpec((tm, wd), lambda i: (i, 0))
    mod_spec = pl.BlockSpec((1, mrows, d), lambda i: (i // tpg, 0, 0))
    const2 = lambda a: pl.BlockSpec(a.shape, lambda i: (0, 0))
    return pl.pallas_call(
        _proj1_body,
        grid=(n // tm,),
        in_specs=[tok(d), mod_spec, mod_spec, const2(g), const2(w), const2(wg), const2(gb)],
        out_specs=[tok(hk), tok(hk), tok(hv), tok(hv), tok(2 * MLSTM_HEADS)],
        out_shape=[jax.ShapeDtypeStruct((n, hk), act_dtype), jax.ShapeDtypeStruct((n, hk), act_dtype),
                   jax.ShapeDtypeStruct((n, hv), act_dtype), jax.ShapeDtypeStruct((n, hv), act_dtype),
                   jax.ShapeDtypeStruct((n, 2 * MLSTM_HEADS), F32)],
        compiler_params=_cparams(("arbitrary",)),
        name="proj1",
    )(x, sh, sc, g, w, wg, gb)


def _mlstm_body(q_ref, k_ref, v_ref, so_ref, gr_ref, gc_ref, ng_ref, c0_ref, n0_ref, m0_ref,
                hh_ref, c_out, n_out, m_out, c_st, n_st, m_st):
    ci = pl.program_id(1)
    nchunk = pl.num_programs(1)
    hcount = MLSTM_HEADS
    dk, dv = MLSTM_DK, MLSTM_DV
    ln = q_ref.shape[1]

    @pl.when(ci == 0)
    def _():
        c_st[...] = c0_ref[0]
        n_st[...] = n0_ref[0]
        m_st[...] = m0_ref[0]

    row = lax.broadcasted_iota(jnp.int32, (ln, ln), 0)
    col = lax.broadcasted_iota(jnp.int32, (ln, ln), 1)
    causal = col <= row
    gr = gr_ref[0]
    gc = gc_ref[0]
    outs = []
    for h in range(hcount):
        ig_r = gr[h:h + 1, :]
        lf_r = gr[hcount + h:hcount + h + 1, :]
        ig_c = gc[:, h:h + 1]
        lf_c = gc[:, hcount + h:hcount + h + 1]
        b_c = jnp.sum(jnp.where(causal, lf_r, 0.0), axis=1, keepdims=True)
        b_r = jnp.sum(jnp.where(row <= col, lf_c, 0.0), axis=0, keepdims=True)
        m_prev = m_st[h:h + 1, 0:1]
        n_prev = n_st[h:h + 1, :]
        c_prev = c_st[h]
        q = q_ref[0, :, h * dk:(h + 1) * dk].astype(BF16)
        k = k_ref[0, :, h * dk:(h + 1) * dk]
        v = v_ref[0, :, h * dv:(h + 1) * dv].astype(BF16)
        logw = jnp.where(causal, b_c - b_r + ig_r, NEG_INF)
        inter = b_c + m_prev
        m_t = jnp.maximum(inter, jnp.max(logw, axis=-1, keepdims=True))
        w_inter = jnp.exp(inter - m_t)
        s = _dot_nt(q, k.astype(BF16)) * jnp.exp(logw - m_t)
        num = w_inter * _dot(q, c_prev.astype(BF16)) + _dot(s.astype(BF16), v)
        den = (w_inter * jnp.sum(q.astype(F32) * n_prev, axis=-1, keepdims=True)
               + jnp.sum(s, axis=-1, keepdims=True))
        hh = num / jnp.maximum(jnp.abs(den), jnp.exp(-m_t))
        hh = _rms(hh)
        outs.append(hh * ng_ref[:, h * dv:(h + 1) * dv] * so_ref[0, :, h * dv:(h + 1) * dv].astype(F32))
        b_last = b_r[:, ln - 1:ln]
        m_new = jnp.maximum(b_last + m_prev, jnp.max(b_last - b_r + ig_r, axis=-1, keepdims=True))
        a = jnp.exp(b_last + m_prev - m_new)
        kw = k.astype(F32) * jnp.exp(b_last - b_c + ig_c - m_new)
        c_st[h] = a * c_prev + _dot_tn(kw.astype(BF16), v)
        n_st[h:h + 1, :] = a * n_prev + jnp.sum(kw, axis=0, keepdims=True)
        m_st[h:h + 1, :] = jnp.broadcast_to(m_new, (1, LANES))
    hh_ref[0] = jnp.concatenate(outs, axis=1).astype(hh_ref.dtype)

    @pl.when(ci == nchunk - 1)
    def _():
        c_out[0] = c_st[...]
        n_out[0] = n_st[...]
        m_out[0] = m_st[...]


def _mlstm(q, k, v, so, gr, gc, ng, c0, n0, m0, ln, out_dtype):
    nb, t, _ = q.shape
    hcount, dk, dv = MLSTM_HEADS, MLSTM_DK, MLSTM_DV
    seq = lambda w: pl.BlockSpec((1, ln, w), lambda b, c: (b, c, 0))
    st3 = lambda a: pl.BlockSpec((1,) + a.shape[1:], lambda b, c: (b,) + (0,) * (a.ndim - 1))
    return pl.pallas_call(
        _mlstm_body,
        grid=(nb, t // ln),
        in_specs=[seq(hcount * dk), seq(hcount * dk), seq(hcount * dv), seq(hcount * dv),
                  pl.BlockSpec((1, 2 * hcount, ln), lambda b, c: (b, 0, c)),
                  seq(2 * hcount),
                  pl.BlockSpec(ng.shape, lambda b, c: (0, 0)),
                  st3(c0), st3(n0), st3(m0)],
        out_specs=[seq(hcount * dv), st3(c0), st3(n0), st3(m0)],
        out_shape=[jax.ShapeDtypeStruct((nb, t, hcount * dv), out_dtype),
                   jax.ShapeDtypeStruct(c0.shape, F32), jax.ShapeDtypeStruct(n0.shape, F32),
                   jax.ShapeDtypeStruct(m0.shape, F32)],
        scratch_shapes=[pltpu.VMEM((hcount, dk, dv), F32), pltpu.VMEM((hcount, dk), F32),
                        pltpu.VMEM((hcount, LANES), F32)],
        compiler_params=_cparams(("arbitrary", "arbitrary")),
        name="mlstm",
    )(q, k, v, so, gr, gc, ng, c0, n0, m0)


def _router_body(x_ref, hh_ref, gm_ref, sh_ref, sc_ref, g_ref, wo_ref, wr_ref, br_ref,
                 x3_ref, h_ref, gd_ref):
    x3 = x_ref[...] + gm_ref[0] * _dot(hh_ref[...].astype(BF16), wo_ref[...])
    x3_ref[...] = x3
    h = _adaln(x3, g_ref[...], sc_ref[0], sh_ref[0])
    h_ref[...] = h.astype(BF16)
    lane = lax.broadcasted_iota(jnp.int32, (h.shape[0], LANES), 1)
    logits = jnp.where(lane < N_EXPERTS, _dot3(h, wr_ref[...]) + br_ref[...], -jnp.inf)
    e = jnp.exp(logits - jnp.max(logits, axis=-1, keepdims=True))
    probs = e / jnp.sum(e, axis=-1, keepdims=True)
    p1 = jnp.max(probs, axis=-1, keepdims=True)
    i1 = jnp.min(jnp.where(probs == p1, lane, LANES), axis=-1, keepdims=True)
    rest = jnp.where(lane == i1, -1.0, probs)
    p2 = jnp.max(rest, axis=-1, keepdims=True)
    i2 = jnp.min(jnp.where(rest == p2, lane, LANES), axis=-1, keepdims=True)
    tot = p1 + p2
    gd_ref[...] = jnp.where(lane == i1, p1 / tot, 0.0) + jnp.where(lane == i2, p2 / tot, 0.0)


def _router(x, hh, mods, tpg, tm, g, wo, wr, br):
    n, d = x.shape
    gm, sh, sc = mods
    mrows = gm.shape[1]
    tok = lambda w: pl.BlockSpec((tm, w), lambda i: (i, 0))
    mod_spec = pl.BlockSpec((1, mrows, d), lambda i: (i // tpg, 0, 0))
    const2 = lambda a: pl.BlockSpec(a.shape, lambda i: (0, 0))
    return pl.pallas_call(
        _router_body,
        grid=(n // tm,),
        in_specs=[tok(d), tok(hh.shape[1]), mod_spec, mod_spec, mod_spec, const2(g), const2(wo),
                  const2(wr), const2(br)],
        out_specs=[tok(d), tok(d), tok(LANES)],
        out_shape=[jax.ShapeDtypeStruct((n, d), F32), jax.ShapeDtypeStruct((n, d), BF16),
                   jax.ShapeDtypeStruct((n, LANES), F32)],
        compiler_params=_cparams(("arbitrary",)),
        name="out1_router",
    )(x, hh, gm, sh, sc, g, wo, wr, br)


def _moe_body(nf, ch, x3_ref, gt_ref, h_ref, gd_ref, fg_ref, w1_ref, w3_ref, w2_ref, y_ref,
              xc_ref, acc_ref, rankc_ref, rankr_ref, gdt_ref, gcol_ref, rcol_ref, cnt_ref):
    e = pl.program_id(1)
    f = pl.program_id(2)
    tm = h_ref.shape[0]
    sub = 256 if tm % 256 == 0 else tm

    @pl.when((e == 0) & (f == 0))
    def _():
        y_ref[...] = jnp.zeros(y_ref.shape, F32)
        gd = gd_ref[...]
        gdt = jnp.transpose(gd)[:16]
        gdt_ref[...] = gdt
        maskc = jnp.where(gd > 0.0, 1.0, 0.0).astype(BF16)
        maskr = jnp.where(gdt > 0.0, 1.0, 0.0).astype(BF16)
        r = lax.broadcasted_iota(jnp.int32, (sub, sub), 0)
        c = lax.broadcasted_iota(jnp.int32, (sub, sub), 1)
        before_c = jnp.where(c < r, 1.0, 0.0).astype(BF16)
        before_r = jnp.where(r < c, 1.0, 0.0).astype(BF16)
        offc = jnp.zeros((1, LANES), F32)
        offr = jnp.zeros((16, 1), F32)
        for sb in range(tm // sub):
            mc = maskc[sb * sub:(sb + 1) * sub]
            mr = maskr[:, sb * sub:(sb + 1) * sub]
            rankc_ref[sb * sub:(sb + 1) * sub, :] = _dot(before_c, mc) + offc
            rankr_ref[:, sb * sub:(sb + 1) * sub] = _dot(mr, before_r) + offr
            offc = offc + jnp.sum(mc.astype(F32), axis=0, keepdims=True)
            offr = offr + jnp.sum(mr.astype(F32), axis=1, keepdims=True)

    @pl.when(f == 0)
    def _():
        lane = lax.broadcasted_iota(jnp.int32, (tm, LANES), 1)
        gcol = jnp.sum(jnp.where(lane == e, gd_ref[...], 0.0), axis=-1, keepdims=True)
        rcol = jnp.sum(jnp.where(lane == e, rankc_ref[...], 0.0), axis=-1, keepdims=True)
        gcol_ref[...] = gcol
        rcol_ref[...] = jnp.where(gcol > 0.0, rcol, -1.0)
        grow = gdt_ref[pl.ds(e, 1), :]
        rrow = jnp.where(grow > 0.0, rankr_ref[pl.ds(e, 1), :], -1.0)
        cnt = jnp.sum(jnp.where(grow > 0.0, 1.0, 0.0)).astype(jnp.int32)
        cnt_ref[0] = cnt

        def compact(c, carry):
            slot = (c * ch + lax.broadcasted_iota(jnp.int32, (ch, 1), 0)).astype(F32)
            pick = jnp.where(rrow == slot, 1.0, 0.0).astype(BF16)
            xc_ref[pl.ds(pl.multiple_of(c * ch, ch), ch), :] = _dot(pick, h_ref[...]).astype(BF16)
            return carry
        lax.fori_loop(0, (cnt + ch - 1) // ch, compact, 0)

    nchunk = (cnt_ref[0] + ch - 1) // ch

    def ffn(c, carry):
        rows = pl.ds(pl.multiple_of(c * ch, ch), ch)
        xc = xc_ref[rows, :]
        a = _dot(xc, w1_ref[0])
        u = _dot(xc, w3_ref[0])
        part = _dot((a * _sigmoid(a) * u).astype(BF16), w2_ref[0])

        @pl.when(f == 0)
        def _():
            acc_ref[rows, :] = part

        @pl.when(f > 0)
        def _():
            acc_ref[rows, :] += part
        return carry
    lax.fori_loop(0, nchunk, ffn, 0)

    @pl.when(f == nf - 1)
    def _():
        def combine(c, carry):
            rows = pl.ds(pl.multiple_of(c * ch, ch), ch)
            slot = (c * ch + lax.broadcasted_iota(jnp.int32, (1, ch), 1)).astype(F32)
            put = jnp.where(rcol_ref[...] == slot, 1.0, 0.0).astype(BF16)
            y_ref[...] += gcol_ref[...] * _dot(put, acc_ref[rows, :].astype(BF16))
            return carry
        lax.fori_loop(0, nchunk, combine, 0)

        @pl.when(e == N_EXPERTS - 1)
        def _():
            x = x3_ref[...] + gt_ref[0] * y_ref[...]
            y_ref[...] = _rms(x) * fg_ref[...]


def _moe(x3, gt, tpg, tm, h, gd, fg, w1, w3, w2, tf, ch):
    n, d = x3.shape
    dfe = w1.shape[2]
    nf = dfe // tf
    mrows = gt.shape[1]
    tok = lambda w: pl.BlockSpec((tm, w), lambda i, e, f: (i, 0))
    return pl.pallas_call(
        functools.partial(_moe_body, nf, ch),
        grid=(n // tm, N_EXPERTS, nf),
        in_specs=[tok(d), pl.BlockSpec((1, mrows, d), lambda i, e, f: (i // tpg, 0, 0)),
                  tok(d), tok(LANES), pl.BlockSpec(fg.shape, lambda i, e, f: (0, 0)),
                  pl.BlockSpec((1, d, tf), lambda i, e, f: (e, 0, f)),
                  pl.BlockSpec((1, d, tf), lambda i, e, f: (e, 0, f)),
                  pl.BlockSpec((1, tf, d), lambda i, e, f: (e, f, 0))],
        out_specs=tok(d),
        out_shape=jax.ShapeDtypeStruct((n, d), F32),
        scratch_shapes=[pltpu.VMEM((tm, d), BF16), pltpu.VMEM((tm, d), F32),
                        pltpu.VMEM((tm, LANES), F32), pltpu.VMEM((16, tm), F32),
                        pltpu.VMEM((16, tm), F32), pltpu.VMEM((tm, 1), F32),
                        pltpu.VMEM((tm, 1), F32), pltpu.SMEM((1,), jnp.int32)],
        compiler_params=_cparams(("arbitrary", "arbitrary", "arbitrary")),
        name="moe_final",
    )(x3, gt, h, gd, fg, w1, w3, w2)


def _rope_tables(pos, dim):
    half = dim // 2
    inv = jnp.power(ROPE_THETA, -jnp.arange(0, dim, 2, dtype=F32) / dim)
    ang = pos[:, None] * inv[None, :]
    cos, sin = jnp.cos(ang), jnp.sin(ang)
    zero = jnp.zeros_like(sin)
    reps = LANES // dim
    tile = lambda a, b: jnp.tile(jnp.concatenate([a, b], axis=1), (1, reps))
    return jnp.stack([tile(cos, cos), tile(-sin, zero), tile(zero, sin)])


def _pick(n, prefs):
    for p in prefs:
        if n % p == 0:
            return p
    return n


def _row(v):
    return v.reshape(1, -1).astype(F32)


def kernel(x_prompt, x_sample, cache_mla_latent, cache_mla_krope, cache_moba_k, cache_moba_v, state_mlstm_C, state_mlstm_n, state_mlstm_m, page_table, c_prompt, c_sample, ln0_mix_g, ada0_mix_w, ada0_mix_b, w_in0, mla_q_norm_g, mla_w_uq, mla_kv_norm_g, mla_w_uk, mla_w_uv, w_out0, ln0_ffn_g, ada0_ffn_w, ada0_ffn_b, ffn0_w1, ffn0_w3, ffn0_w2, ln1_mix_g, ada1_mix_w, ada1_mix_b, w_in1, mlstm_gate_bias, mlstm_norm_g, w_out1, ln1_ffn_g, ada1_ffn_w, ada1_ffn_b, moe_w_router, moe_b_router, moe_w1, moe_w3, moe_w2, final_norm_g):
    nb, t, d = x_prompt.shape
    nreq, s_new, _ = x_sample.shape
    n_pages = page_table.shape[1]
    p_len = n_pages * PAGE_SIZE
    n_p, n_s = nb * t, nreq * s_new
    tm_p = _pick(t, (512, 256))
    tm_s = _pick(n_s, (256, 128, 64))
    tpg_p = t // tm_p

    c_all = jnp.concatenate([c_prompt, c_sample], axis=0)
    pad = (-c_all.shape[0]) % 8
    c_all = jnp.pad(c_all, ((0, pad), (0, 0)))

    def mods(w, b):
        m = _modulation(c_all, w, b)
        parts_p, parts_s = [], []
        for i in range(3):
            col = m[:, i * d:(i + 1) * d]
            parts_p.append(col[:nb].reshape(nb, 1, d))
            parts_s.append(jnp.repeat(col[nb:nb + nreq], s_new, axis=0).reshape(n_s // tm_s, tm_s, d))
        return parts_p, parts_s

    m0m_p, m0m_s = mods(ada0_mix_w, ada0_mix_b)
    m0f_p, m0f_s = mods(ada0_ffn_w, ada0_ffn_b)
    m1m_p, m1m_s = mods(ada1_mix_w, ada1_mix_b)
    m1f_p, m1f_s = mods(ada1_ffn_w, ada1_ffn_b)

    sp = [0, 256, 384, 416, 928, 1184, 1440]
    pieces = [w_in0[:, sp[i]:sp[i + 1]] for i in range(6)]
    pieces[2] = jnp.pad(pieces[2], ((0, 0), (0, LANES - MLA_ROPE)))
    w0 = jnp.concatenate(pieces, axis=1).astype(BF16)
    wq = _fold_q_weights(mla_w_uq, mla_w_uk)
    wuv = mla_w_uv.transpose(1, 0, 2).astype(BF16)
    hv_mla = MLA_HEADS * MLA_V
    woa = w_out0[:hv_mla].astype(BF16)
    wob = w_out0[hv_mla:].astype(BF16)
    f1, f3, f2 = ffn0_w1.astype(BF16), ffn0_w3.astype(BF16), ffn0_w2.astype(BF16)
    hk = MLSTM_HEADS * MLSTM_DK
    hvl = MLSTM_HEADS * MLSTM_DV
    w1main = w_in1[:, :2 * hk + 2 * hvl].astype(BF16)
    w1gate = jnp.pad(w_in1[:, 2 * hk + 2 * hvl:], ((0, 0), (0, LANES - 2 * MLSTM_HEADS)))
    gbias = jnp.pad(mlstm_gate_bias, (0, LANES - 2 * MLSTM_HEADS)).reshape(1, LANES)
    wo1 = w_out1.astype(BF16)
    wr = jnp.pad(moe_w_router, ((0, 0), (0, LANES - N_EXPERTS)))
    br = jnp.pad(moe_b_router, (0, LANES - N_EXPERTS)).reshape(1, LANES)
    e1, e3, e2 = moe_w1.astype(BF16), moe_w3.astype(BF16), moe_w2.astype(BF16)
    tf0 = _pick(f1.shape[1], (1408, 512, 256, 128))
    tfe = _pick(e1.shape[2], (896, 512, 256, 128))

    pos_p = jnp.arange(t).astype(F32)
    pos_s = jnp.tile((p_len + jnp.arange(s_new)).astype(F32), tm_s // s_new)

    def layer0(x, mm, mf, tpg, tm, pos, emit_kmean, attend):
        o = _proj0(x, (mm[0], mm[1]), tpg, tm, _row(ln0_mix_g), w0, wq, _row(mla_q_norm_g),
                   _row(mla_kv_norm_g), _rope_tables(pos, MLA_ROPE), _rope_tables(pos, MOBA_HD),
                   emit_kmean)
        oa, ob = attend(o)
        x2 = _ffn0(x, oa, ob, (mm[2], mf[0], mf[1], mf[2]), tpg, tm, _row(ln0_ffn_g), woa, wob,
                   f1, f3, f2, tf0)
        return x2, o[2], o[3], o[5], o[6]

    def attend_prompt(o):
        qf, kf, _, _, mq, _, _, mkb, mvb, km = o
        oa = _mla_prompt(qf, kf, wuv, nb, t, _pick(t, (256, 128)))
        ob = _moba_prompt(mq, mkb, mvb, km.reshape(n_p // MOBA_BLOCK, 256), nb, t)
        return oa, ob

    def attend_sample(o):
        qf, kf, _, _, mq, mk, mv, _, _ = o
        q = qf.reshape(MLA_HEADS, nreq, s_new, MLA_KW).transpose(1, 0, 2, 3)
        q = q.reshape(nreq, MLA_HEADS * s_new, MLA_KW)
        kn = kf.astype(F32).reshape(nreq, s_new, MLA_KW)
        oa = _mla_sample(page_table, q, kn, wuv, cache_mla_latent, cache_mla_krope)
        q5 = mq.reshape(nreq, s_new, 2, 2, 2, MOBA_HD).transpose(0, 2, 3, 4, 1, 5)
        z5 = jnp.zeros_like(q5[:, :, 0])
        qm = jnp.stack([jnp.stack([q5[:, :, 0], z5], axis=-2),
                        jnp.stack([z5, q5[:, :, 1]], axis=-2)], axis=2)
        qm = qm.reshape(nreq, 2, 4 * s_new, LANES)
        ob = _moba_sample(page_table, qm, mk.reshape(nreq, s_new, 256), mv.reshape(nreq, s_new, 256),
                          cache_moba_k.reshape(-1, PAGE_SIZE, 256), cache_moba_v.reshape(-1, PAGE_SIZE, 256))
        return oa.reshape(n_s, -1), ob.reshape(n_s, -1)

    def layer1(x2, mm, mf, tpg, tm, nseq, tlen, ln, c0, n0, m0, act_dtype):
        q, k, v, so, gates = _proj1(x2, (mm[0], mm[1]), tpg, tm, _row(ln1_mix_g), w1main, w1gate,
                                    gbias, act_dtype)
        gc = gates.reshape(nseq, tlen, 2 * MLSTM_HEADS)
        gr = gc.transpose(0, 2, 1)
        r3 = lambda a: a.reshape(nseq, tlen, -1)
        m0b = jnp.broadcast_to(m0.astype(F32)[:, :, None], (nseq, MLSTM_HEADS, LANES))
        hh, c_new, n_new, m_new = _mlstm(r3(q), r3(k), r3(v), r3(so), gr, gc, _row(mlstm_norm_g),
                                         c0.astype(F32), n0.astype(F32), m0b, ln, act_dtype)
        x3, h4, gd = _router(x2, hh.reshape(nseq * tlen, -1), (mm[2], mf[0], mf[1]), tpg, tm,
                             _row(ln1_ffn_g), wo1, wr, br)
        tmo = _pick(x3.shape[0], (1024, 512, 256))
        tpgo = max(1, (tpg * tm) // tmo)
        gate = mf[2] if mf[2].shape[1] == 1 else mf[2].reshape(-1, tmo, d)
        y = _moe(x3, gate, tpgo, tmo, h4, gd, _row(final_norm_g), e1, e3, e2, tfe, _pick(tmo, (256,)))
        return y, c_new, n_new, m_new[:, :, 0]

    x2p, p_lat, p_kr, p_k, p_v = layer0(x_prompt.reshape(n_p, d), m0m_p, m0f_p, tpg_p, tm_p, pos_p,
                                        True, attend_prompt)
    x2s, s_lat, s_kr, s_k, s_v = layer0(x_sample.reshape(n_s, d), m0m_s, m0f_s, 1, tm_s, pos_s,
                                        False, attend_sample)

    zc = jnp.zeros((nb, MLSTM_HEADS, MLSTM_DK, MLSTM_DV), F32)
    zn = jnp.zeros((nb, MLSTM_HEADS, MLSTM_DK), F32)
    zm = jnp.zeros((nb, MLSTM_HEADS), F32)
    yp, p_c, p_n, p_m = layer1(x2p, m1m_p, m1f_p, tpg_p, tm_p, nb, t, _pick(t, (256, 128, 64)),
                               zc, zn, zm, BF16)
    ys, s_c, s_n, s_m = layer1(x2s, m1m_s, m1f_s, 1, tm_s, nreq, s_new, s_new,
                               state_mlstm_C, state_mlstm_n, state_mlstm_m, F32)

    kv4 = lambda a, lead: a.reshape(lead + (MOBA_KV_HEADS, MOBA_HD))
    return (yp.reshape(nb, t, d), ys.reshape(nreq, s_new, d),
            p_lat.reshape(nb, t, -1), p_kr.reshape(nb, t, -1), kv4(p_k, (nb, t)), kv4(p_v, (nb, t)),
            p_c, p_n, p_m,
            s_lat.reshape(nreq, s_new, -1), s_kr.reshape(nreq, s_new, -1),
            kv4(s_k, (nreq, s_new)), kv4(s_v, (nreq, s_new)), s_c, s_n, s_m)
```
